```python
import jax, jax.numpy as jnp
from jax import lax
import numpy as np

D_MODEL = 1024
BATCH = 4
SEQ = 8192
DEPTH = 4

CHUNK = 64
D_MIX = D_MODEL
D_A = D_MIX // 2
D_B = D_MIX - D_A
A_HEADS = 4
A_BLOCK = 128
A_HEAD_DIM = D_A // A_HEADS
B_HEADS = 4
B_DK = D_B // 2
B_DV = D_B
B_HEAD_K = B_DK // B_HEADS
B_HEAD_V = B_DV // B_HEADS
B_GATE_RANK = 16
B_GATE_TAU = 16.0
D_FF = 4 * D_MODEL
D_IN = 2 * D_A + 2 * B_DK + 2 * B_DV + B_GATE_RANK
EPS = 1e-6

kernel_name = "hybrid_gmlp_gla_sqrelu_sandwich"


def rmsnorm(x, g):
    xf = x.astype(jnp.float32)
    xf = xf * lax.rsqrt(jnp.mean(xf * xf, axis=-1, keepdims=True) + EPS)
    return (xf * g.astype(jnp.float32)).astype(x.dtype)


def layernorm(x, g, b):
    xf = x.astype(jnp.float32)
    mu = jnp.mean(xf, axis=-1, keepdims=True)
    xc = xf - mu
    xf = xc * lax.rsqrt(jnp.mean(xc * xc, axis=-1, keepdims=True) + EPS)
    return (xf * g.astype(jnp.float32) + b.astype(jnp.float32)).astype(x.dtype)


def gmlp_mixer(zu, zv, ln_g, ln_b, ws, bs, out_g):
    bsz, s, _ = zu.shape
    u = jax.nn.gelu(zu)
    v = layernorm(jax.nn.gelu(zv), ln_g, ln_b)
    n_blk = s // A_BLOCK
    v = v.reshape(bsz, n_blk, A_BLOCK, A_HEADS, A_HEAD_DIM)
    pos = jnp.arange(A_BLOCK) // CHUNK
    mask = pos[:, None] >= pos[None, :]
    w = jnp.where(mask[None], ws, jnp.zeros_like(ws))
    mixed = jnp.einsum('hij,bnjhc->bnihc', w, v) + jnp.transpose(bs)[None, None, :, :, None]
    out = u * mixed.reshape(bsz, s, D_A)
    return rmsnorm(out, out_g)


def gla_mixer(zq, zk, zv, zg, zlr, gate_w, gate_b, out_g):
    dt = zq.dtype
    bsz, s, _ = zq.shape
    n_c = s // CHUNK
    f32 = jnp.float32
    q = zq.astype(f32) * (B_HEAD_K ** -0.5)
    k = zk.astype(f32)
    v = zv.astype(f32)
    log_a = jax.nn.log_sigmoid(zlr.astype(f32) @ gate_w.astype(f32) + gate_b.astype(f32)) / B_GATE_TAU

    def to_chunks(t, d):
        return jnp.transpose(t.reshape(bsz, n_c, CHUNK, B_HEADS, d), (0, 3, 1, 2, 4))

    q, k, log_a = to_chunks(q, B_HEAD_K), to_chunks(k, B_HEAD_K), to_chunks(log_a, B_HEAD_K)
    v = to_chunks(v, B_HEAD_V)
    cum = jnp.cumsum(log_a, axis=-2)
    last = cum[..., -1:, :]
    qe = q * jnp.exp(cum)
    ke = k * jnp.exp(-cum)
    kd = k * jnp.exp(last - cum)

    tril = jnp.tril(jnp.ones((CHUNK, CHUNK), dtype=bool))
    scores = jnp.einsum('bhnid,bhnjd->bhnij', qe, ke)
    scores = jnp.where(tril, scores, 0.0)
    o_intra = jnp.einsum('bhnij,bhnjv->bhniv', scores, v)

    upd = jnp.einsum('bhnjd,bhnjv->bhndv', kd, v)
    dec = jnp.exp(last[..., 0, :])

    def step(state, inp):
        d_c, u_c = inp
        return d_c[..., None] * state + u_c, state

    s0 = jnp.zeros((bsz, B_HEADS, B_HEAD_K, B_HEAD_V), f32)
    _, s_prev = lax.scan(step, s0, (jnp.moveaxis(dec, 2, 0), jnp.moveaxis(upd, 2, 0)))
    s_prev = jnp.moveaxis(s_prev, 0, 2)
    o_inter = jnp.einsum('bhnid,bhndv->bhniv', qe, s_prev)

    o = jnp.transpose(o_intra + o_inter, (0, 2, 3, 1, 4))
    o = o.reshape(bsz, s, B_HEADS, B_HEAD_V)
    o = rmsnorm(o, out_g.reshape(B_HEADS, B_HEAD_V)).reshape(bsz, s, B_DV)
    return (o * jax.nn.silu(zg.astype(f32))).astype(dt)


def setup_inputs(seed: int = 0) -> dict:
    key = jax.random.key(seed)
    ks = jax.random.split(key, 20)
    n = jax.random.normal

    def gain(k, d):
        return 1.0 + 0.05 * n(k, (DEPTH, d), jnp.float32)

    return {
        "x": n(ks[0], (BATCH, SEQ, D_MODEL), jnp.float32),
        "pre_mix_g": gain(ks[1], D_MODEL),
        "w_in": n(ks[2], (DEPTH, D_MODEL, D_IN), jnp.float32) * D_MODEL ** -0.5,
        "gmlp_ln_g": gain(ks[3], D_A),
        "gmlp_ln_b": 0.02 * n(ks[4], (DEPTH, D_A), jnp.float32),
        "gmlp_ws": n(ks[5], (DEPTH, A_HEADS, A_BLOCK, A_BLOCK), jnp.float32) * A_BLOCK ** -0.5,
        "gmlp_bs": 1.0 + 0.1 * n(ks[6], (DEPTH, A_HEADS, A_BLOCK), jnp.float32),
        "gmlp_out_g": gain(ks[7], D_A),
        "gla_gate_w": n(ks[8], (DEPTH, B_GATE_RANK, B_DK), jnp.float32) * B_GATE_RANK ** -0.5,
        "gla_gate_b": 0.5 * n(ks[9], (DEPTH, B_DK), jnp.float32),
        "gla_out_g": gain(ks[10], B_DV),
        "w_out": n(ks[11], (DEPTH, D_MIX, D_MODEL), jnp.float32) * D_MIX ** -0.5,
        "post_mix_g": gain(ks[12], D_MODEL),
        "pre_ff_g": gain(ks[13], D_MODEL),
        "w_ff1": n(ks[14], (DEPTH, D_MODEL, D_FF), jnp.float32) * D_MODEL ** -0.5,
        "w_ff2": n(ks[15], (DEPTH, D_FF, D_MODEL), jnp.float32) * D_FF ** -0.5,
        "post_ff_g": gain(ks[16], D_MODEL),
    }


def reference(x, pre_mix_g, w_in, gmlp_ln_g, gmlp_ln_b, gmlp_ws, gmlp_bs, gmlp_out_g,
              gla_gate_w, gla_gate_b, gla_out_g, w_out, post_mix_g, pre_ff_g,
              w_ff1, w_ff2, post_ff_g):
    cuts = [D_A, 2 * D_A, 2 * D_A + B_DK, 2 * D_A + 2 * B_DK,
            2 * D_A + 2 * B_DK + B_DV, 2 * D_A + 2 * B_DK + 2 * B_DV]
    h = x
    for l in range(DEPTH):
        y = rmsnorm(h, pre_mix_g[l])
        z = y @ w_in[l]
        zu, zv, zq, zk, zvb, zg, zlr = jnp.split(z, cuts, axis=-1)
        a = gmlp_mixer(zu, zv, gmlp_ln_g[l], gmlp_ln_b[l], gmlp_ws[l], gmlp_bs[l], gmlp_out_g[l])
        b = gla_mixer(zq, zk, zvb, zg, zlr, gla_gate_w[l], gla_gate_b[l], gla_out_g[l])
        m = jnp.concatenate([a, b], axis=-1) @ w_out[l]
        h = h + rmsnorm(m, post_mix_g[l])
        f = rmsnorm(h, pre_ff_g[l]) @ w_ff1[l]
        f = jnp.square(jax.nn.relu(f)) @ w_ff2[l]
        h = h + rmsnorm(f, post_ff_g[l])
    return h
```

```python
import functools

import jax
import jax.numpy as jnp
from jax import lax
from jax.experimental import pallas as pl
from jax.experimental.pallas import tpu as pltpu

D_MODEL = 1024
CHUNK = 64
D_A = 512
A_HEADS = 4
A_BLOCK = 128
A_HEAD_DIM = 128
B_HEADS = 4
B_DK = 256
B_DV = 512
B_HEAD_K = 64
B_HEAD_V = 128
B_GATE_RANK = 16
B_GATE_TAU = 16.0
D_FF = 4096
D_IN = 2 * D_A + 2 * B_DK + 2 * B_DV + B_GATE_RANK
EPS = 1e-6

C_UV = 0
C_QK = 2 * D_A
C_VG = C_QK + 2 * B_DK
C_LR = C_VG + 2 * B_DV

MIXER_TILE = 256
FFN_TILE = 512
VMEM_LIMIT_BYTES = 56 * 1024 * 1024

F32 = jnp.float32
BF16 = jnp.bfloat16


def _rms(x, g):
    ms = jnp.mean(x * x, axis=-1, keepdims=True)
    return x * lax.rsqrt(ms + EPS) * g


def _gelu(x):
    return jax.nn.gelu(x, approximate=True)


def _dot(a, b):
    return jnp.dot(a, b, preferred_element_type=F32)


def _dot_nt(a, b):
    return lax.dot_general(a, b, (((1,), (1,)), ((), ())), preferred_element_type=F32)


def _dot_tn(a, b):
    return lax.dot_general(a, b, (((0,), (0,)), ((), ())), preferred_element_type=F32)


def _split_bf16(x):
    hi = x.astype(BF16)
    lo = (x - hi.astype(F32)).astype(BF16)
    return hi, lo


def _block_diag_mask(rows, cols, row_blk, col_blk):
    r = lax.broadcasted_iota(jnp.int32, (rows, cols), 0) // row_blk
    c = lax.broadcasted_iota(jnp.int32, (rows, cols), 1) // col_blk
    return r == c


def _mixer_kernel(h_ref, pre_g_ref, w_in_ref, ln_g_ref, ln_b_ref, ws_ref, bs_ref,
                  a_out_g_ref, gate_w_ref, gate_b_ref, b_out_g_ref, w_out_ref,
                  post_g_ref, o_ref, state_ref):
    tile = h_ref.shape[0]

    @pl.when(pl.program_id(1) == 0)
    def _():
        state_ref[...] = jnp.zeros_like(state_ref)

    x = h_ref[...]
    y = _rms(x, pre_g_ref[...]).astype(BF16)

    z_uv = _dot(y, w_in_ref[:, C_UV:C_QK])
    z_qk = _dot(y, w_in_ref[:, C_QK:C_VG])
    z_vg = _dot(y, w_in_ref[:, C_VG:C_LR])
    z_lr = _dot(y, w_in_ref[:, C_LR:D_IN])

    u = _gelu(z_uv[:, :D_A])
    gv = _gelu(z_uv[:, D_A:])
    mu = jnp.mean(gv, axis=-1, keepdims=True)
    gc = gv - mu
    var = jnp.mean(gc * gc, axis=-1, keepdims=True)
    v = (gc * lax.rsqrt(var + EPS) * ln_g_ref[...] + ln_b_ref[...]).astype(BF16)

    pos_i = lax.broadcasted_iota(jnp.int32, (A_BLOCK, A_HEADS * A_BLOCK), 0) // CHUNK
    pos_j = (lax.broadcasted_iota(jnp.int32, (A_BLOCK, A_HEADS * A_BLOCK), 1) % A_BLOCK) // CHUNK
    ws = jnp.where(pos_i >= pos_j, ws_ref[...], jnp.zeros((), BF16))
    pair_mask = _block_diag_mask(2 * A_BLOCK, 2 * A_HEAD_DIM, A_BLOCK, A_HEAD_DIM)
    mixed_blocks = []
    for blk in range(tile // A_BLOCK):
        vb = v[blk * A_BLOCK:(blk + 1) * A_BLOCK]
        pairs = []
        for p in range(A_HEADS // 2):
            vp = vb[:, p * 2 * A_HEAD_DIM:(p + 1) * 2 * A_HEAD_DIM]
            rhs = jnp.where(pair_mask, jnp.concatenate([vp, vp], axis=0), jnp.zeros((), BF16))
            pairs.append(_dot(ws[:, p * 2 * A_BLOCK:(p + 1) * 2 * A_BLOCK], rhs))
        mixed_blocks.append(jnp.concatenate(pairs, axis=1) + bs_ref[...])
    mixed = jnp.concatenate(mixed_blocks, axis=0)
    a = _rms(u * mixed, a_out_g_ref[...]).astype(BF16)

    gate = _dot(z_lr.astype(BF16), gate_w_ref[...]) + gate_b_ref[...]
    log_a = jax.nn.log_sigmoid(gate) * (1.0 / B_GATE_TAU)
    tri = (lax.broadcasted_iota(jnp.int32, (CHUNK, CHUNK), 0)
           >= lax.broadcasted_iota(jnp.int32, (CHUNK, CHUNK), 1))
    tri_bf = tri.astype(BF16)
    causal = (lax.broadcasted_iota(jnp.int32, (CHUNK, B_HEADS * CHUNK), 0)
              >= lax.broadcasted_iota(jnp.int32, (CHUNK, B_HEADS * CHUNK), 1) % CHUNK)
    k_mask = _block_diag_mask(B_HEADS * CHUNK, B_DK, CHUNK, B_HEAD_K)
    v_mask = _block_diag_mask(B_HEADS * CHUNK, B_DV, CHUNK, B_HEAD_V)
    s_mask = _block_diag_mask(B_DK, B_DV, B_HEAD_K, B_HEAD_V)
    ones_bf = jnp.ones((CHUNK, B_HEAD_V), BF16)
    state = state_ref[...]
    o_chunks = []
    for c in range(tile // CHUNK):
        rows = slice(c * CHUNK, (c + 1) * CHUNK)
        la_hi, la_lo = _split_bf16(log_a[rows])
        cum = _dot(tri_bf, la_hi) + _dot(tri_bf, la_lo)
        last = cum[CHUNK - 1:CHUNK]
        q = z_qk[rows, :B_DK] * (B_HEAD_K ** -0.5)
        k = z_qk[rows, B_DK:]
        vv = z_vg[rows, :B_DV].astype(BF16)
        qe = (q * jnp.exp(cum)).astype(BF16)
        ke = (k * jnp.exp(-cum)).astype(BF16)
        kd = (k * jnp.exp(last - cum)).astype(BF16)
        ke_bd = jnp.where(k_mask, jnp.concatenate([ke] * B_HEADS, axis=0), jnp.zeros((), BF16))
        scores = _dot_nt(qe, ke_bd)
        scores = jnp.where(causal, scores, 0.0).astype(BF16)
        v_bd = jnp.where(v_mask, jnp.concatenate([vv] * B_HEADS, axis=0), jnp.zeros((), BF16))
        o_c = _dot(scores, v_bd) + _dot(qe, state.astype(BF16))
        o_chunks.append(o_c)
        dec_col = jnp.exp(_dot_tn(la_hi, ones_bf) + _dot_tn(la_lo, ones_bf))
        upd = _dot_tn(kd, vv)
        state = jnp.concatenate([dec_col] * B_HEADS, axis=1) * state + jnp.where(s_mask, upd, 0.0)
    state_ref[...] = state
    o = jnp.concatenate(o_chunks, axis=0)
    heads = []
    for hd in range(B_HEADS):
        lanes = slice(hd * B_HEAD_V, (hd + 1) * B_HEAD_V)
        heads.append(_rms(o[:, lanes], b_out_g_ref[:, lanes]))
    zg = z_vg[:, B_DV:]
    b = (jnp.concatenate(heads, axis=1) * (zg * jax.nn.sigmoid(zg))).astype(BF16)

    m = _dot(jnp.concatenate([a, b], axis=1), w_out_ref[...])
    o_ref[...] = x + _rms(m, post_g_ref[...])


def _ffn_kernel(h_ref, pre_g_ref, w1_ref, w2_ref, post_g_ref, o_ref):
    x = h_ref[...]
    y = _rms(x, pre_g_ref[...]).astype(BF16)
    f = jnp.maximum(_dot(y, w1_ref[...]), 0.0)
    f = (f * f).astype(BF16)
    o_ref[...] = x + _rms(_dot(f, w2_ref[...]), post_g_ref[...])


def _layer_spec(layer, shape):
    zeros = (0,) * len(shape)
    return pl.BlockSpec((None,) + tuple(shape), lambda b, t: (layer,) + zeros,
                        pipeline_mode=pl.Buffered(1))


def _tile_spec(tile):
    return pl.BlockSpec((None, tile, D_MODEL), lambda b, t: (b, t, 0))


def _compiler_params():
    return pltpu.CompilerParams(dimension_semantics=("arbitrary", "arbitrary"),
                                vmem_limit_bytes=VMEM_LIMIT_BYTES)


def _mixer_call(layer, h, p):
    bsz, seq, _ = h.shape
    in_specs = [
        _tile_spec(MIXER_TILE),
        _layer_spec(layer, (1, D_MODEL)),
        _layer_spec(layer, (D_MODEL, D_IN)),
        _layer_spec(layer, (1, D_A)),
        _layer_spec(layer, (1, D_A)),
        _layer_spec(layer, (A_BLOCK, A_HEADS * A_BLOCK)),
        _layer_spec(layer, (A_BLOCK, D_A)),
        _layer_spec(layer, (1, D_A)),
        _layer_spec(layer, (B_GATE_RANK, B_DK)),
        _layer_spec(layer, (1, B_DK)),
        _layer_spec(layer, (1, B_DV)),
        _layer_spec(layer, (D_MODEL, D_MODEL)),
        _layer_spec(layer, (1, D_MODEL)),
    ]
    return pl.pallas_call(
        _mixer_kernel,
        grid=(bsz, seq // MIXER_TILE),
        in_specs=in_specs,
        out_specs=_tile_spec(MIXER_TILE),
        out_shape=jax.ShapeDtypeStruct(h.shape, h.dtype),
        scratch_shapes=[pltpu.VMEM((B_DK, B_DV), F32)],
        compiler_params=_compiler_params(),
        name=f"mixer_l{layer}",
    )(h, p["pre_mix_g"], p["w_in"], p["gmlp_ln_g"], p["gmlp_ln_b"], p["gmlp_ws"], p["gmlp_bs"],
      p["gmlp_out_g"], p["gla_gate_w"], p["gla_gate_b"], p["gla_out_g"], p["w_out"],
      p["post_mix_g"])


def _ffn_call(layer, h, p):
    bsz, seq, _ = h.shape
    in_specs = [
        _tile_spec(FFN_TILE),
        _layer_spec(layer, (1, D_MODEL)),
        _layer_spec(layer, (D_MODEL, D_FF)),
        _layer_spec(layer, (D_FF, D_MODEL)),
        _layer_spec(layer, (1, D_MODEL)),
    ]
    return pl.pallas_call(
        _ffn_kernel,
        grid=(bsz, seq // FFN_TILE),
        in_specs=in_specs,
        out_specs=_tile_spec(FFN_TILE),
        out_shape=jax.ShapeDtypeStruct(h.shape, h.dtype),
        compiler_params=_compiler_params(),
        name=f"ffn_l{layer}",
    )(h, p["pre_ff_g"], p["w_ff1"], p["w_ff2"], p["post_ff_g"])


def kernel(x, pre_mix_g, w_in, gmlp_ln_g, gmlp_ln_b, gmlp_ws, gmlp_bs, gmlp_out_g,
           gla_gate_w, gla_gate_b, gla_out_g, w_out, post_mix_g, pre_ff_g,
           w_ff1, w_ff2, post_ff_g):
    depth = w_in.shape[0]
    row = lambda g: g[:, None, :]
    p = {
        "pre_mix_g": row(pre_mix_g),
        "w_in": w_in.astype(BF16),
        "gmlp_ln_g": row(gmlp_ln_g),
        "gmlp_ln_b": row(gmlp_ln_b),
        "gmlp_ws": jnp.transpose(gmlp_ws, (0, 2, 1, 3)).reshape(depth, A_BLOCK, A_HEADS * A_BLOCK).astype(BF16),
        "gmlp_bs": jnp.repeat(jnp.transpose(gmlp_bs, (0, 2, 1)), A_HEAD_DIM, axis=2),
        "gmlp_out_g": row(gmlp_out_g),
        "gla_gate_w": gla_gate_w.astype(BF16),
        "gla_gate_b": row(gla_gate_b),
        "gla_out_g": row(gla_out_g),
        "w_out": w_out.astype(BF16),
        "post_mix_g": row(post_mix_g),
        "pre_ff_g": row(pre_ff_g),
        "w_ff1": w_ff1.astype(BF16),
        "w_ff2": w_ff2.astype(BF16),
        "post_ff_g": row(post_ff_g),
    }
    h = x
    for layer in range(depth):
        h = _mixer_call(layer, h, p)
        h = _ffn_call(layer, h, p)
    return h
```

```python
import functools

import jax
import jax.numpy as jnp
from jax import lax
from jax.experimental import pallas as pl
from jax.experimental.pallas import tpu as pltpu

D_MODEL = 1024
CHUNK = 64
D_A = 512
A_HEADS = 4
A_BLOCK = 128
A_HEAD_DIM = 128
B_HEADS = 4
B_DK = 256
B_DV = 512
B_HEAD_K = 64
B_HEAD_V = 128
B_GATE_RANK = 16
B_GATE_TAU = 16.0
D_FF = 4096
D_IN = 2 * D_A + 2 * B_DK + 2 * B_DV + B_GATE_RANK
EPS = 1e-6

C_UV = 0
C_QK = 2 * D_A
C_VG = C_QK + 2 * B_DK
C_LR = C_VG + 2 * B_DV

TILE = 256
FF_CHUNK = 512
N_FF_CHUNKS = D_FF // FF_CHUNK
VMEM_LIMIT_BYTES = 56 * 1024 * 1024

F32 = jnp.float32
BF16 = jnp.bfloat16


def _rms(x, g):
    ms = jnp.mean(x * x, axis=-1, keepdims=True)
    return x * lax.rsqrt(ms + EPS) * g


def _gelu(x):
    return jax.nn.gelu(x, approximate=True)


def _dot(a, b):
    return jnp.dot(a, b, preferred_element_type=F32)


def _dot_nt(a, b):
    return lax.dot_general(a, b, (((1,), (1,)), ((), ())), preferred_element_type=F32)


def _split_bf16(x):
    hi = x.astype(BF16)
    lo = (x - hi.astype(F32)).astype(BF16)
    return hi, lo


def _block_diag_mask(rows, cols, row_blk, col_blk):
    r = lax.broadcasted_iota(jnp.int32, (rows, cols), 0) // row_blk
    c = lax.broadcasted_iota(jnp.int32, (rows, cols), 1) // col_blk
    return r == c


def _zeros_bf():
    return jnp.zeros((), BF16)


def _layer_kernel(tiles_per_seq,
                  h_ref, pre_g_ref, w_in_ref, ln_g_ref, ln_b_ref, ws_ref, bs_ref,
                  a_out_g_ref, gate_w_ref, gate_b_ref, b_out_g_ref, w_out_ref,
                  post_g_ref, ff_pre_g_ref, w1_ref, w2_ref, ff_post_g_ref,
                  o_ref, state_ref, h1_ref, y2_ref):
    step = pl.program_id(0)
    slot = step % 2

    @pl.when(step == 0)
    def _():
        h1_ref[1] = jnp.zeros(h1_ref.shape[1:], F32)
        y2_ref[1] = jnp.zeros(y2_ref.shape[1:], BF16)
        state_ref[...] = jnp.zeros_like(state_ref)

    y2 = y2_ref[1 - slot]
    ff = {"pending": None, "acc": None, "j1": 0, "j2": 0}

    def ffn_w1():
        j = ff["j1"]
        f = jnp.dot(y2, w1_ref[:, j * FF_CHUNK:(j + 1) * FF_CHUNK], preferred_element_type=F32)
        f = jnp.maximum(f, 0.0)
        ff["j1"] = j + 1
        return (f * f).astype(BF16)

    def ffn_w2(f):
        j = ff["j2"]
        part = jnp.dot(f, w2_ref[j * FF_CHUNK:(j + 1) * FF_CHUNK, :], preferred_element_type=F32)
        ff["acc"] = part if ff["acc"] is None else ff["acc"] + part
        ff["j2"] = j + 1

    def ffn_advance():
        prev = ff["pending"]
        ff["pending"] = ffn_w1() if ff["j1"] < N_FF_CHUNKS else None
        if prev is not None:
            ffn_w2(prev)

    ffn_advance()
    x = h_ref[...]
    y = _rms(x, pre_g_ref[...]).astype(BF16)
    z_lr = jnp.dot(y, w_in_ref[:, C_LR:D_IN], preferred_element_type=F32)
    z_qk = jnp.dot(y, w_in_ref[:, C_QK:C_VG], preferred_element_type=F32)
    gate = _dot(z_lr.astype(BF16), gate_w_ref[...]) + gate_b_ref[...]
    ffn_advance()
    log_a = jax.nn.log_sigmoid(gate) * (1.0 / B_GATE_TAU)
    n_chunks = TILE // CHUNK
    chunk_rows = [slice(c * CHUNK, (c + 1) * CHUNK) for c in range(n_chunks)]
    tri_bf = (lax.broadcasted_iota(jnp.int32, (CHUNK, CHUNK), 0)
              >= lax.broadcasted_iota(jnp.int32, (CHUNK, CHUNK), 1)).astype(BF16)
    la_hi, la_lo = _split_bf16(jnp.concatenate([log_a[r] for r in chunk_rows], axis=1))
    cum_all = _dot(tri_bf, la_hi) + _dot(tri_bf, la_lo)
    z_vg = jnp.dot(y, w_in_ref[:, C_VG:C_LR], preferred_element_type=F32)
    z_uv = jnp.dot(y, w_in_ref[:, C_UV:C_QK], preferred_element_type=F32)

    causal = (lax.broadcasted_iota(jnp.int32, (CHUNK, B_HEADS * CHUNK), 0)
              >= lax.broadcasted_iota(jnp.int32, (CHUNK, B_HEADS * CHUNK), 1) % CHUNK)
    k_mask = _block_diag_mask(B_HEADS * CHUNK, B_DK, CHUNK, B_HEAD_K)
    v_mask = _block_diag_mask(B_HEADS * CHUNK, B_DV, CHUNK, B_HEAD_V)
    s_mask = _block_diag_mask(B_DK, B_DV, B_HEAD_K, B_HEAD_V)
    qe, ke, kd_t, vv, dec_col = [], [], [], [], []
    for c, r in enumerate(chunk_rows):
        cum = cum_all[:, c * B_DK:(c + 1) * B_DK]
        last = cum[CHUNK - 1:CHUNK]
        k = z_qk[r, B_DK:]
        qe.append((z_qk[r, :B_DK] * (B_HEAD_K ** -0.5) * jnp.exp(cum)).astype(BF16))
        ke.append((k * jnp.exp(-cum)).astype(BF16))
        kd_t.append((k * jnp.exp(last - cum)).T.astype(BF16))
        vv.append(z_vg[r, :B_DV].astype(BF16))
        dec_col.append(jnp.broadcast_to(jnp.exp(last), (B_HEAD_V, B_DK)).T)
    upd = [jnp.concatenate(
        [_dot(kd_t[c][hd * B_HEAD_K:(hd + 1) * B_HEAD_K], vv[c][:, hd * B_HEAD_V:(hd + 1) * B_HEAD_V])
         for hd in range(B_HEADS)], axis=0) for c in range(n_chunks)]
    scores = []
    for c in range(n_chunks):
        ke_bd = jnp.where(k_mask, jnp.concatenate([ke[c]] * B_HEADS, axis=0), _zeros_bf())
        scores.append(jnp.where(causal, _dot_nt(qe[c], ke_bd), 0.0).astype(BF16))
    ffn_advance()
    first_of_seq = (step % tiles_per_seq) == 0
    states = [jnp.where(first_of_seq, 0.0, state_ref[...])]
    for c in range(n_chunks):
        states.append(dec_col[c] * states[c] + upd[c])
    state_ref[...] = states[n_chunks]
    o_chunks = []
    for c in range(n_chunks):
        v_bd = jnp.where(v_mask, jnp.concatenate([vv[c]] * B_HEADS, axis=0), _zeros_bf())
        s_bd = jnp.where(s_mask, jnp.concatenate([states[c].astype(BF16)] * B_HEADS, axis=1),
                         _zeros_bf())
        o_chunks.append(_dot(jnp.concatenate([scores[c], qe[c]], axis=1),
                             jnp.concatenate([v_bd, s_bd], axis=0)))

    u = _gelu(z_uv[:, :D_A])
    gv = _gelu(z_uv[:, D_A:])
    mu = jnp.mean(gv, axis=-1, keepdims=True)
    gc = gv - mu
    var = jnp.mean(gc * gc, axis=-1, keepdims=True)
    v = (gc * lax.rsqrt(var + EPS) * ln_g_ref[...] + ln_b_ref[...]).astype(BF16)
    pos_i = lax.broadcasted_iota(jnp.int32, (A_BLOCK, A_HEADS * A_BLOCK), 0) // CHUNK
    pos_j = (lax.broadcasted_iota(jnp.int32, (A_BLOCK, A_HEADS * A_BLOCK), 1) % A_BLOCK) // CHUNK
    ws = jnp.where(pos_i >= pos_j, ws_ref[...], _zeros_bf())
    pair_mask = _block_diag_mask(2 * A_BLOCK, 2 * A_HEAD_DIM, A_BLOCK, A_HEAD_DIM)
    mixed_blocks = []
    for blk in range(TILE // A_BLOCK):
        vb = v[blk * A_BLOCK:(blk + 1) * A_BLOCK]
        pairs = []
        for p in range(A_HEADS // 2):
            vp = vb[:, p * 2 * A_HEAD_DIM:(p + 1) * 2 * A_HEAD_DIM]
            rhs = jnp.where(pair_mask, jnp.concatenate([vp, vp], axis=0), _zeros_bf())
            pairs.append(_dot(ws[:, p * 2 * A_BLOCK:(p + 1) * 2 * A_BLOCK], rhs))
        mixed_blocks.append(jnp.concatenate(pairs, axis=1) + bs_ref[...])
    ffn_advance()
    ffn_advance()
    mixed = jnp.concatenate(mixed_blocks, axis=0)
    a = _rms(u * mixed, a_out_g_ref[...]).astype(BF16)
    o = jnp.concatenate(o_chunks, axis=0)
    heads = []
    for hd in range(B_HEADS):
        lanes = slice(hd * B_HEAD_V, (hd + 1) * B_HEAD_V)
        heads.append(_rms(o[:, lanes], b_out_g_ref[:, lanes]))
    zg = z_vg[:, B_DV:]
    b = (jnp.concatenate(heads, axis=1) * (zg * jax.nn.sigmoid(zg))).astype(BF16)

    m = jnp.dot(jnp.concatenate([a, b], axis=1), w_out_ref[...], preferred_element_type=F32)
    ffn_advance()
    ffn_advance()
    h1 = x + _rms(m, post_g_ref[...])
    h1_ref[slot] = h1
    y2_ref[slot] = _rms(h1, ff_pre_g_ref[...]).astype(BF16)

    while ff["j2"] < N_FF_CHUNKS:
        ffn_advance()
    o_ref[...] = h1_ref[1 - slot] + _rms(ff["acc"], ff_post_g_ref[...])


def _layer_spec(layer, shape):
    zeros = (0,) * len(shape)
    return pl.BlockSpec((None,) + tuple(shape), lambda s: (layer,) + zeros,
                        pipeline_mode=pl.Buffered(1))


def _layer_call(layer, h, p, tiles_per_seq):
    n_tiles = h.shape[0] // TILE
    in_specs = [
        pl.BlockSpec((TILE, D_MODEL), lambda s: (jnp.minimum(s, n_tiles - 1), 0)),
        _layer_spec(layer, (1, D_MODEL)),
        _layer_spec(layer, (D_MODEL, D_IN)),
        _layer_spec(layer, (1, D_A)),
        _layer_spec(layer, (1, D_A)),
        _layer_spec(layer, (A_BLOCK, A_HEADS * A_BLOCK)),
        _layer_spec(layer, (A_BLOCK, D_A)),
        _layer_spec(layer, (1, D_A)),
        _layer_spec(layer, (B_GATE_RANK, B_DK)),
        _layer_spec(layer, (1, B_DK)),
        _layer_spec(layer, (1, B_DV)),
        _layer_spec(layer, (D_MODEL, D_MODEL)),
        _layer_spec(layer, (1, D_MODEL)),
        _layer_spec(layer, (1, D_MODEL)),
        _layer_spec(layer, (D_MODEL, D_FF)),
        _layer_spec(layer, (D_FF, D_MODEL)),
        _layer_spec(layer, (1, D_MODEL)),
    ]
    return pl.pallas_call(
        functools.partial(_layer_kernel, tiles_per_seq),
        grid=(n_tiles + 1,),
        in_specs=in_specs,
        out_specs=pl.BlockSpec((TILE, D_MODEL), lambda s: (jnp.maximum(s - 1, 0), 0)),
        out_shape=jax.ShapeDtypeStruct(h.shape, h.dtype),
        scratch_shapes=[pltpu.VMEM((B_DK, B_HEAD_V), F32),
                        pltpu.VMEM((2, TILE, D_MODEL), F32),
                        pltpu.VMEM((2, TILE, D_MODEL), BF16)],
        compiler_params=pltpu.CompilerParams(dimension_semantics=("arbitrary",),
                                             vmem_limit_bytes=VMEM_LIMIT_BYTES),
        name=f"layer{layer}",
    )(h, p["pre_mix_g"], p["w_in"], p["gmlp_ln_g"], p["gmlp_ln_b"], p["gmlp_ws"], p["gmlp_bs"],
      p["gmlp_out_g"], p["gla_gate_w"], p["gla_gate_b"], p["gla_out_g"], p["w_out"],
      p["post_mix_g"], p["pre_ff_g"], p["w_ff1"], p["w_ff2"], p["post_ff_g"])


def kernel(x, pre_mix_g, w_in, gmlp_ln_g, gmlp_ln_b, gmlp_ws, gmlp_bs, gmlp_out_g,
           gla_gate_w, gla_gate_b, gla_out_g, w_out, post_mix_g, pre_ff_g,
           w_ff1, w_ff2, post_ff_g):
    depth = w_in.shape[0]
    bsz, seq, _ = x.shape
    row = lambda g: g[:, None, :]
    p = {
        "pre_mix_g": row(pre_mix_g),
        "w_in": w_in.astype(BF16),
        "gmlp_ln_g": row(gmlp_ln_g),
        "gmlp_ln_b": row(gmlp_ln_b),
        "gmlp_ws": jnp.transpose(gmlp_ws, (0, 2, 1, 3)).reshape(depth, A_BLOCK, A_HEADS * A_BLOCK).astype(BF16),
        "gmlp_bs": jnp.repeat(jnp.transpose(gmlp_bs, (0, 2, 1)), A_HEAD_DIM, axis=2),
        "gmlp_out_g": row(gmlp_out_g),
        "gla_gate_w": gla_gate_w.astype(BF16),
        "gla_gate_b": row(gla_gate_b),
        "gla_out_g": row(gla_out_g),
        "w_out": w_out.astype(BF16),
        "post_mix_g": row(post_mix_g),
        "pre_ff_g": row(pre_ff_g),
        "w_ff1": w_ff1.astype(BF16),
        "w_ff2": w_ff2.astype(BF16),
        "post_ff_g": row(post_ff_g),
    }
    h = x.reshape(bsz * seq, D_MODEL)
    for layer in range(depth):
        h = _layer_call(layer, h, p, seq // TILE)
    return h.reshape(bsz, seq, D_MODEL)
```

```python
import functools

import jax
import jax.numpy as jnp
from jax import lax
from jax.experimental import pallas as pl
from jax.experimental.pallas import tpu as pltpu

D_MODEL = 1024
CHUNK = 64
D_A = 512
A_HEADS = 4
A_BLOCK = 128
A_HEAD_DIM = 128
B_HEADS = 4
B_DK = 256
B_DV = 512
B_HEAD_K = 64
B_HEAD_V = 128
B_GATE_RANK = 16
B_GATE_TAU = 16.0
D_FF = 4096
D_IN = 2 * D_A + 2 * B_DK + 2 * B_DV + B_GATE_RANK
EPS = 1e-6

C_UV = 0
C_QK = 2 * D_A
C_VG = C_QK + 2 * B_DK
C_LR = C_VG + 2 * B_DV

TILE = 512
FF_CHUNK = 512
N_FF_CHUNKS = D_FF // FF_CHUNK
VMEM_LIMIT_BYTES = 56 * 1024 * 1024

F32 = jnp.float32
BF16 = jnp.bfloat16


def _rms(x, g):
    ms = jnp.mean(x * x, axis=-1, keepdims=True)
    return x * lax.rsqrt(ms + EPS) * g


def _gelu(x):
    return jax.nn.gelu(x, approximate=True)


def _dot(a, b):
    return jnp.dot(a, b, preferred_element_type=F32)


def _dot_nt(a, b):
    return lax.dot_general(a, b, (((1,), (1,)), ((), ())), preferred_element_type=F32)


def _split_bf16(x):
    hi = x.astype(BF16)
    lo = (x - hi.astype(F32)).astype(BF16)
    return hi, lo


def _block_diag_mask(rows, cols, row_blk, col_blk):
    r = lax.broadcasted_iota(jnp.int32, (rows, cols), 0) // row_blk
    c = lax.broadcasted_iota(jnp.int32, (rows, cols), 1) // col_blk
    return r == c


def _zeros_bf():
    return jnp.zeros((), BF16)


def _layer_kernel(tiles_per_seq,
                  h_ref, pre_g_ref, w_in_ref, ln_g_ref, ln_b_ref, ws_ref, bs_ref,
                  a_out_g_ref, gate_w_ref, gate_b_ref, b_out_g_ref, w_out_ref,
                  post_g_ref, ff_pre_g_ref, w1_ref, w2_ref, ff_post_g_ref,
                  o_ref, state_ref, h1_ref, y2_ref):
    step = pl.program_id(0)
    slot = step % 2

    @pl.when(step == 0)
    def _():
        h1_ref[1] = jnp.zeros(h1_ref.shape[1:], F32)
        y2_ref[1] = jnp.zeros(y2_ref.shape[1:], BF16)
        state_ref[...] = jnp.zeros_like(state_ref)

    y2 = y2_ref[1 - slot]
    ff = {"pending": None, "acc": None, "j1": 0, "j2": 0}

    def ffn_w1():
        j = ff["j1"]
        f = jnp.dot(y2, w1_ref[:, j * FF_CHUNK:(j + 1) * FF_CHUNK], preferred_element_type=F32)
        f = jnp.maximum(f, 0.0)
        ff["j1"] = j + 1
        return (f * f).astype(BF16)

    def ffn_w2(f):
        j = ff["j2"]
        part = jnp.dot(f, w2_ref[j * FF_CHUNK:(j + 1) * FF_CHUNK, :], preferred_element_type=F32)
        ff["acc"] = part if ff["acc"] is None else ff["acc"] + part
        ff["j2"] = j + 1

    def ffn_advance():
        prev = ff["pending"]
        ff["pending"] = ffn_w1() if ff["j1"] < N_FF_CHUNKS else None
        if prev is not None:
            ffn_w2(prev)

    ffn_advance()
    x = h_ref[...]
    y = _rms(x, pre_g_ref[...]).astype(BF16)
    z_lr = jnp.dot(y, w_in_ref[:, C_LR:D_IN], preferred_element_type=F32)
    z_qk = jnp.dot(y, w_in_ref[:, C_QK:C_VG], preferred_element_type=F32)
    gate = _dot(z_lr.astype(BF16), gate_w_ref[...]) + gate_b_ref[...]
    ffn_advance()
    log_a = jax.nn.log_sigmoid(gate) * (1.0 / B_GATE_TAU)
    n_chunks = TILE // CHUNK
    chunk_rows = [slice(c * CHUNK, (c + 1) * CHUNK) for c in range(n_chunks)]
    tri_bf = (lax.broadcasted_iota(jnp.int32, (CHUNK, CHUNK), 0)
              >= lax.broadcasted_iota(jnp.int32, (CHUNK, CHUNK), 1)).astype(BF16)
    la_hi, la_lo = _split_bf16(jnp.concatenate([log_a[r] for r in chunk_rows], axis=1))
    cum_all = _dot(tri_bf, la_hi) + _dot(tri_bf, la_lo)
    z_vg = jnp.dot(y, w_in_ref[:, C_VG:C_LR], preferred_element_type=F32)
    ffn_advance()

    causal = (lax.broadcasted_iota(jnp.int32, (CHUNK, B_HEADS * CHUNK), 0)
              >= lax.broadcasted_iota(jnp.int32, (CHUNK, B_HEADS * CHUNK), 1) % CHUNK)
    k_mask = _block_diag_mask(B_HEADS * CHUNK, B_DK, CHUNK, B_HEAD_K)
    pair_mask_v = _block_diag_mask(2 * CHUNK, 2 * B_HEAD_V, CHUNK, B_HEAD_V)
    qe, ke_t, kd_t, vv, dec_col = [], [], [], [], []
    for c, r in enumerate(chunk_rows):
        cum = cum_all[:, c * B_DK:(c + 1) * B_DK]
        last = cum[CHUNK - 1:CHUNK]
        qe.append((z_qk[r, :B_DK] * (B_HEAD_K ** -0.5) * jnp.exp(cum)).astype(BF16))
        ke = z_qk[r, B_DK:] * jnp.exp(-cum)
        ke_t.append(jnp.concatenate([ke] * B_HEADS, axis=0).T)
        vv.append(z_vg[r, :B_DV].astype(BF16))
        dec_col.append(jnp.broadcast_to(jnp.exp(last), (B_HEAD_V, B_DK)).T)
        kd_t.append((ke_t[c][:, :CHUNK] * dec_col[c][:, :CHUNK]).astype(BF16))
    upd = [jnp.concatenate(
        [_dot(kd_t[c][hd * B_HEAD_K:(hd + 1) * B_HEAD_K], vv[c][:, hd * B_HEAD_V:(hd + 1) * B_HEAD_V])
         for hd in range(B_HEADS)], axis=0) for c in range(n_chunks)]
    scores = []
    for c in range(n_chunks):
        ke_bd_t = jnp.where(k_mask, ke_t[c], 0.0).astype(BF16)
        scores.append(jnp.where(causal, _dot(qe[c], ke_bd_t), 0.0).astype(BF16))
    z_uv = jnp.dot(y, w_in_ref[:, C_UV:C_QK], preferred_element_type=F32)
    first_of_seq = (step % tiles_per_seq) == 0
    states = [jnp.where(first_of_seq, 0.0, state_ref[...])]
    for c in range(n_chunks):
        states.append(dec_col[c] * states[c] + upd[c])
    state_ref[...] = states[n_chunks]
    o_chunks = []
    for c in range(n_chunks):
        state_bf = states[c].astype(BF16)
        pair_out = []
        for p in range(B_HEADS // 2):
            jd = slice(p * 2 * CHUNK, (p + 1) * 2 * CHUNK)
            vp = vv[c][:, p * 2 * B_HEAD_V:(p + 1) * 2 * B_HEAD_V]
            v_bd = jnp.where(pair_mask_v, jnp.concatenate([vp, vp], axis=0), _zeros_bf())
            sp = state_bf[jd]
            s_bd = jnp.where(pair_mask_v, jnp.concatenate([sp, sp], axis=1), _zeros_bf())
            pair_out.append(_dot(jnp.concatenate([scores[c][:, jd], qe[c][:, jd]], axis=1),
                                 jnp.concatenate([v_bd, s_bd], axis=0)))
        o_chunks.append(jnp.concatenate(pair_out, axis=1))
    ffn_advance()
    o = jnp.concatenate(o_chunks, axis=0)
    heads = []
    for hd in range(B_HEADS):
        lanes = slice(hd * B_HEAD_V, (hd + 1) * B_HEAD_V)
        heads.append(_rms(o[:, lanes], b_out_g_ref[:, lanes]))
    zg = z_vg[:, B_DV:]
    b = (jnp.concatenate(heads, axis=1) * (zg * jax.nn.sigmoid(zg))).astype(BF16)

    u = _gelu(z_uv[:, :D_A])
    gv = _gelu(z_uv[:, D_A:])
    mu = jnp.mean(gv, axis=-1, keepdims=True)
    gc = gv - mu
    var = jnp.mean(gc * gc, axis=-1, keepdims=True)
    v = (gc * lax.rsqrt(var + EPS) * ln_g_ref[...] + ln_b_ref[...]).astype(BF16)
    ffn_advance()
    pos_i = lax.broadcasted_iota(jnp.int32, (A_BLOCK, A_HEADS * A_BLOCK), 0) // CHUNK
    pos_j = (lax.broadcasted_iota(jnp.int32, (A_BLOCK, A_HEADS * A_BLOCK), 1) % A_BLOCK) // CHUNK
    ws = jnp.where(pos_i >= pos_j, ws_ref[...], _zeros_bf())
    pair_mask = _block_diag_mask(2 * A_BLOCK, 2 * A_HEAD_DIM, A_BLOCK, A_HEAD_DIM)
    mixed_blocks = []
    for blk in range(TILE // A_BLOCK):
        vb = v[blk * A_BLOCK:(blk + 1) * A_BLOCK]
        pairs = []
        for p in range(A_HEADS // 2):
            vp = vb[:, p * 2 * A_HEAD_DIM:(p + 1) * 2 * A_HEAD_DIM]
            rhs = jnp.where(pair_mask, jnp.concatenate([vp, vp], axis=0), _zeros_bf())
            pairs.append(_dot(ws[:, p * 2 * A_BLOCK:(p + 1) * 2 * A_BLOCK], rhs))
        mixed_blocks.append(jnp.concatenate(pairs, axis=1) + bs_ref[...])
    ffn_advance()
    mixed = jnp.concatenate(mixed_blocks, axis=0)
    a = _rms(u * mixed, a_out_g_ref[...]).astype(BF16)
    ffn_advance()

    m = jnp.dot(jnp.concatenate([a, b], axis=1), w_out_ref[...], preferred_element_type=F32)
    ffn_advance()
    h1 = x + _rms(m, post_g_ref[...])
    h1_ref[slot] = h1
    y2_ref[slot] = _rms(h1, ff_pre_g_ref[...]).astype(BF16)

    while ff["j2"] < N_FF_CHUNKS:
        ffn_advance()
    o_ref[...] = h1_ref[1 - slot] + _rms(ff["acc"], ff_post_g_ref[...])


def _layer_spec(layer, shape):
    zeros = (0,) * len(shape)
    return pl.BlockSpec((None,) + tuple(shape), lambda s: (layer,) + zeros,
                        pipeline_mode=pl.Buffered(1))


def _layer_call(layer, h, p, tiles_per_seq):
    n_tiles = h.shape[0] // TILE
    in_specs = [
        pl.BlockSpec((TILE, D_MODEL), lambda s: (jnp.minimum(s, n_tiles - 1), 0)),
        _layer_spec(layer, (1, D_MODEL)),
        _layer_spec(layer, (D_MODEL, D_IN)),
        _layer_spec(layer, (1, D_A)),
        _layer_spec(layer, (1, D_A)),
        _layer_spec(layer, (A_BLOCK, A_HEADS * A_BLOCK)),
        _layer_spec(layer, (A_BLOCK, D_A)),
        _layer_spec(layer, (1, D_A)),
        _layer_spec(layer, (B_GATE_RANK, B_DK)),
        _layer_spec(layer, (1, B_DK)),
        _layer_spec(layer, (1, B_DV)),
        _layer_spec(layer, (D_MODEL, D_MODEL)),
        _layer_spec(layer, (1, D_MODEL)),
        _layer_spec(layer, (1, D_MODEL)),
        _layer_spec(layer, (D_MODEL, D_FF)),
        _layer_spec(layer, (D_FF, D_MODEL)),
        _layer_spec(layer, (1, D_MODEL)),
    ]
    return pl.pallas_call(
        functools.partial(_layer_kernel, tiles_per_seq),
        grid=(n_tiles + 1,),
        in_specs=in_specs,
        out_specs=pl.BlockSpec((TILE, D_MODEL), lambda s: (jnp.maximum(s - 1, 0), 0)),
        out_shape=jax.ShapeDtypeStruct(h.shape, h.dtype),
        scratch_shapes=[pltpu.VMEM((B_DK, B_HEAD_V), F32),
                        pltpu.VMEM((2, TILE, D_MODEL), F32),
                        pltpu.VMEM((2, TILE, D_MODEL), BF16)],
        compiler_params=pltpu.CompilerParams(dimension_semantics=("arbitrary",),
                                             vmem_limit_bytes=VMEM_LIMIT_BYTES),
        name=f"layer{layer}",
    )(h, p["pre_mix_g"], p["w_in"], p["gmlp_ln_g"], p["gmlp_ln_b"], p["gmlp_ws"], p["gmlp_bs"],
      p["gmlp_out_g"], p["gla_gate_w"], p["gla_gate_b"], p["gla_out_g"], p["w_out"],
      p["post_mix_g"], p["pre_ff_g"], p["w_ff1"], p["w_ff2"], p["post_ff_g"])


def kernel(x, pre_mix_g, w_in, gmlp_ln_g, gmlp_ln_b, gmlp_ws, gmlp_bs, gmlp_out_g,
           gla_gate_w, gla_gate_b, gla_out_g, w_out, post_mix_g, pre_ff_g,
           w_ff1, w_ff2, post_ff_g):
    depth = w_in.shape[0]
    bsz, seq, _ = x.shape
    row = lambda g: g[:, None, :]
    p = {
        "pre_mix_g": row(pre_mix_g),
        "w_in": w_in.astype(BF16),
        "gmlp_ln_g": row(gmlp_ln_g),
        "gmlp_ln_b": row(gmlp_ln_b),
        "gmlp_ws": jnp.transpose(gmlp_ws, (0, 2, 1, 3)).reshape(depth, A_BLOCK, A_HEADS * A_BLOCK).astype(BF16),
        "gmlp_bs": jnp.repeat(jnp.transpose(gmlp_bs, (0, 2, 1)), A_HEAD_DIM, axis=2),
        "gmlp_out_g": row(gmlp_out_g),
        "gla_gate_w": gla_gate_w.astype(BF16),
        "gla_gate_b": row(gla_gate_b),
        "gla_out_g": row(gla_out_g),
        "w_out": w_out.astype(BF16),
        "post_mix_g": row(post_mix_g),
        "pre_ff_g": row(pre_ff_g),
        "w_ff1": w_ff1.astype(BF16),
        "w_ff2": w_ff2.astype(BF16),
        "post_ff_g": row(post_ff_g),
    }
    h = x.reshape(bsz * seq, D_MODEL)
    for layer in range(depth):
        h = _layer_call(layer, h, p, seq // TILE)
    return h.reshape(bsz, seq, D_MODEL)
```

```python
import functools

import jax
import jax.numpy as jnp
from jax import lax
from jax.experimental import pallas as pl
from jax.experimental.pallas import tpu as pltpu

D_MODEL = 1024
CHUNK = 64
D_A = 512
A_HEADS = 4
A_BLOCK = 128
A_HEAD_DIM = 128
B_HEADS = 4
B_DK = 256
B_DV = 512
B_HEAD_K = 64
B_HEAD_V = 128
B_GATE_RANK = 16
B_GATE_TAU = 16.0
D_FF = 4096
D_IN = 2 * D_A + 2 * B_DK + 2 * B_DV + B_GATE_RANK
EPS = 1e-6

C_UV = 0
C_QK = 2 * D_A
C_VG = C_QK + 2 * B_DK
C_LR = C_VG + 2 * B_DV

TILE = 512
FF_CHUNK = 512
N_FF_CHUNKS = D_FF // FF_CHUNK
W_PAD = 16
VMEM_LIMIT_BYTES = 56 * 1024 * 1024

F32 = jnp.float32
BF16 = jnp.bfloat16


def _rms(x, g):
    ms = jnp.mean(x * x, axis=-1, keepdims=True)
    return x * lax.rsqrt(ms + EPS) * g


def _gelu(x):
    return jax.nn.gelu(x, approximate=True)


def _dot(a, b):
    return jnp.dot(a, b, preferred_element_type=F32)


def _dot_nt(a, b):
    return lax.dot_general(a, b, (((1,), (1,)), ((), ())), preferred_element_type=F32)


def _split_bf16(x):
    hi = x.astype(BF16)
    lo = (x - hi.astype(F32)).astype(BF16)
    return hi, lo


def _block_diag_mask(rows, cols, row_blk, col_blk):
    r = lax.broadcasted_iota(jnp.int32, (rows, cols), 0) // row_blk
    c = lax.broadcasted_iota(jnp.int32, (rows, cols), 1) // col_blk
    return r == c


def _zeros_bf():
    return jnp.zeros((), BF16)


def _layer_kernel(tiles_per_seq,
                  h_ref, pre_g_ref, w_in_ref, ln_g_ref, ln_b_ref, ws_ref, bs_ref,
                  a_out_g_ref, gate_w_ref, gate_b_ref, b_out_g_ref, w_out_ref,
                  post_g_ref, ff_pre_g_ref, w1_ref, w2_ref, ff_post_g_ref,
                  o_ref, state_ref, h1_ref, y2_ref):
    step = pl.program_id(0)
    slot = step % 2

    @pl.when(step == 0)
    def _():
        h1_ref[1] = jnp.zeros(h1_ref.shape[1:], F32)
        y2_ref[1] = jnp.zeros(y2_ref.shape[1:], BF16)
        state_ref[...] = jnp.zeros_like(state_ref)

    y2 = y2_ref[1 - slot]
    ff = {"hidden": {}, "acc": None}

    def ffn_w1(j):
        f = jnp.dot(y2, w1_ref[:, j * FF_CHUNK:(j + 1) * FF_CHUNK], preferred_element_type=F32)
        f = jnp.maximum(f, 0.0)
        ff["hidden"][j] = (f * f).astype(BF16)

    def ffn_w2(i):
        f = jnp.concatenate([ff["hidden"].pop(2 * i), ff["hidden"].pop(2 * i + 1)], axis=1)
        part = jnp.dot(f, w2_ref[2 * i * FF_CHUNK:(2 * i + 2) * FF_CHUNK, :D_MODEL],
                       preferred_element_type=F32)
        ff["acc"] = part if ff["acc"] is None else ff["acc"] + part

    ffn_queue = []
    for i in range(N_FF_CHUNKS // 2):
        ffn_queue += [(ffn_w1, 2 * i), (ffn_w1, 2 * i + 1)]
        if i > 0:
            ffn_queue.insert(len(ffn_queue) - 1, (ffn_w2, i - 1))
    ffn_queue.append((ffn_w2, N_FF_CHUNKS // 2 - 1))

    def ffn_advance(n=1):
        for _ in range(min(n, len(ffn_queue))):
            fn, arg = ffn_queue.pop(0)
            fn(arg)

    ffn_advance()
    x = h_ref[...]
    y = _rms(x, pre_g_ref[...]).astype(BF16)
    z_lr = jnp.dot(y, w_in_ref[:, C_LR:D_IN], preferred_element_type=F32)
    z_qk = jnp.dot(y, w_in_ref[:, C_QK:C_VG], preferred_element_type=F32)
    gate = _dot(z_lr.astype(BF16), gate_w_ref[...]) + gate_b_ref[...]
    ffn_advance()
    log_a = jax.nn.log_sigmoid(gate) * (1.0 / B_GATE_TAU)
    n_chunks = TILE // CHUNK
    chunk_rows = [slice(c * CHUNK, (c + 1) * CHUNK) for c in range(n_chunks)]
    tri_bf = (lax.broadcasted_iota(jnp.int32, (CHUNK, CHUNK), 0)
              >= lax.broadcasted_iota(jnp.int32, (CHUNK, CHUNK), 1)).astype(BF16)
    la_hi, la_lo = _split_bf16(jnp.concatenate([log_a[r] for r in chunk_rows], axis=1))
    cum_all = _dot(tri_bf, la_hi) + _dot(tri_bf, la_lo)
    z_vg = jnp.dot(y, w_in_ref[:, C_VG:C_LR], preferred_element_type=F32)
    ffn_advance(2)

    causal = (lax.broadcasted_iota(jnp.int32, (CHUNK, B_HEADS * CHUNK), 0)
              >= lax.broadcasted_iota(jnp.int32, (CHUNK, B_HEADS * CHUNK), 1) % CHUNK)
    k_mask = _block_diag_mask(B_HEADS * CHUNK, B_DK, CHUNK, B_HEAD_K)
    pair_mask_v = _block_diag_mask(2 * CHUNK, 2 * B_HEAD_V, CHUNK, B_HEAD_V)
    qe, ke_t, kd_t, vv, dec_col = [], [], [], [], []
    for c, r in enumerate(chunk_rows):
        cum = cum_all[:, c * B_DK:(c + 1) * B_DK]
        last = cum[CHUNK - 1:CHUNK]
        qe.append((z_qk[r, :B_DK] * (B_HEAD_K ** -0.5) * jnp.exp(cum)).astype(BF16))
        ke = z_qk[r, B_DK:] * jnp.exp(-cum)
        ke_t.append(jnp.concatenate([ke] * B_HEADS, axis=0).T)
        vv.append(z_vg[r, :B_DV].astype(BF16))
        dec_col.append(jnp.broadcast_to(jnp.exp(last), (B_HEAD_V, B_DK)).T)
        kd_t.append((ke_t[c][:, :CHUNK] * dec_col[c][:, :CHUNK]).astype(BF16))
    upd = [jnp.concatenate(
        [_dot(kd_t[c][hd * B_HEAD_K:(hd + 1) * B_HEAD_K], vv[c][:, hd * B_HEAD_V:(hd + 1) * B_HEAD_V])
         for hd in range(B_HEADS)], axis=0) for c in range(n_chunks)]
    scores = []
    for c in range(n_chunks):
        ke_bd_t = jnp.where(k_mask, ke_t[c], 0.0).astype(BF16)
        scores.append(jnp.where(causal, _dot(qe[c], ke_bd_t), 0.0).astype(BF16))
    z_uv = jnp.dot(y, w_in_ref[:, C_UV:C_QK], preferred_element_type=F32)
    first_of_seq = (step % tiles_per_seq) == 0
    states = [jnp.where(first_of_seq, 0.0, state_ref[...])]
    for c in range(n_chunks):
        states.append(dec_col[c] * states[c] + upd[c])
    state_ref[...] = states[n_chunks]
    o_chunks = []
    for c in range(n_chunks):
        state_bf = states[c].astype(BF16)
        pair_out = []
        for p in range(B_HEADS // 2):
            jd = slice(p * 2 * CHUNK, (p + 1) * 2 * CHUNK)
            vp = vv[c][:, p * 2 * B_HEAD_V:(p + 1) * 2 * B_HEAD_V]
            v_bd = jnp.where(pair_mask_v, jnp.concatenate([vp, vp], axis=0), _zeros_bf())
            sp = state_bf[jd]
            s_bd = jnp.where(pair_mask_v, jnp.concatenate([sp, sp], axis=1), _zeros_bf())
            pair_out.append(_dot(jnp.concatenate([scores[c][:, jd], qe[c][:, jd]], axis=1),
                                 jnp.concatenate([v_bd, s_bd], axis=0)))
        o_chunks.append(jnp.concatenate(pair_out, axis=1))
    ffn_advance()
    o = jnp.concatenate(o_chunks, axis=0)
    heads = []
    for hd in range(B_HEADS):
        lanes = slice(hd * B_HEAD_V, (hd + 1) * B_HEAD_V)
        heads.append(_rms(o[:, lanes], b_out_g_ref[:, lanes]))
    zg = z_vg[:, B_DV:]
    b = (jnp.concatenate(heads, axis=1) * (zg * jax.nn.sigmoid(zg))).astype(BF16)

    u = _gelu(z_uv[:, :D_A])
    gv = _gelu(z_uv[:, D_A:])
    mu = jnp.mean(gv, axis=-1, keepdims=True)
    gc = gv - mu
    var = jnp.mean(gc * gc, axis=-1, keepdims=True)
    v = (gc * lax.rsqrt(var + EPS) * ln_g_ref[...] + ln_b_ref[...]).astype(BF16)
    ffn_advance(2)
    pos_i = lax.broadcasted_iota(jnp.int32, (A_BLOCK, A_HEADS * A_BLOCK), 0) // CHUNK
    pos_j = (lax.broadcasted_iota(jnp.int32, (A_BLOCK, A_HEADS * A_BLOCK), 1) % A_BLOCK) // CHUNK
    ws = jnp.where(pos_i >= pos_j, ws_ref[...], _zeros_bf())
    pair_mask = _block_diag_mask(2 * A_BLOCK, 2 * A_HEAD_DIM, A_BLOCK, A_HEAD_DIM)
    mixed_blocks = []
    for blk in range(TILE // A_BLOCK):
        vb = v[blk * A_BLOCK:(blk + 1) * A_BLOCK]
        pairs = []
        for p in range(A_HEADS // 2):
            vp = vb[:, p * 2 * A_HEAD_DIM:(p + 1) * 2 * A_HEAD_DIM]
            rhs = jnp.where(pair_mask, jnp.concatenate([vp, vp], axis=0), _zeros_bf())
            pairs.append(_dot(ws[:, p * 2 * A_BLOCK:(p + 1) * 2 * A_BLOCK], rhs))
        mixed_blocks.append(jnp.concatenate(pairs, axis=1) + bs_ref[...])
    ffn_advance()
    mixed = jnp.concatenate(mixed_blocks, axis=0)
    a = _rms(u * mixed, a_out_g_ref[...]).astype(BF16)
    ffn_advance(2)

    m = jnp.dot(jnp.concatenate([a, b], axis=1), w_out_ref[:, :D_MODEL], preferred_element_type=F32)
    ffn_advance()
    h1 = x + _rms(m, post_g_ref[...])
    h1_ref[slot] = h1
    y2_ref[slot] = _rms(h1, ff_pre_g_ref[...]).astype(BF16)

    ffn_advance(len(ffn_queue))
    o_ref[...] = h1_ref[1 - slot] + _rms(ff["acc"], ff_post_g_ref[...])


def _layer_spec(layer, shape):
    zeros = (0,) * len(shape)
    return pl.BlockSpec((None,) + tuple(shape), lambda s: (layer,) + zeros,
                        pipeline_mode=pl.Buffered(1))


def _layer_call(layer, h, p, tiles_per_seq):
    n_tiles = h.shape[0] // TILE
    in_specs = [
        pl.BlockSpec((TILE, D_MODEL), lambda s: (jnp.minimum(s, n_tiles - 1), 0)),
        _layer_spec(layer, (1, D_MODEL)),
        _layer_spec(layer, (D_MODEL, D_IN)),
        _layer_spec(layer, (1, D_A)),
        _layer_spec(layer, (1, D_A)),
        _layer_spec(layer, (A_BLOCK, A_HEADS * A_BLOCK)),
        _layer_spec(layer, (A_BLOCK, D_A)),
        _layer_spec(layer, (1, D_A)),
        _layer_spec(layer, (B_GATE_RANK, B_DK)),
        _layer_spec(layer, (1, B_DK)),
        _layer_spec(layer, (1, B_DV)),
        _layer_spec(layer, (D_MODEL, D_MODEL + W_PAD)),
        _layer_spec(layer, (1, D_MODEL)),
        _layer_spec(layer, (1, D_MODEL)),
        _layer_spec(layer, (D_MODEL, D_FF + W_PAD)),
        _layer_spec(layer, (D_FF, D_MODEL + W_PAD)),
        _layer_spec(layer, (1, D_MODEL)),
    ]
    return pl.pallas_call(
        functools.partial(_layer_kernel, tiles_per_seq),
        grid=(n_tiles + 1,),
        in_specs=in_specs,
        out_specs=pl.BlockSpec((TILE, D_MODEL), lambda s: (jnp.maximum(s - 1, 0), 0)),
        out_shape=jax.ShapeDtypeStruct(h.shape, h.dtype),
        scratch_shapes=[pltpu.VMEM((B_DK, B_HEAD_V), F32),
                        pltpu.VMEM((2, TILE, D_MODEL), F32),
                        pltpu.VMEM((2, TILE, D_MODEL), BF16)],
        compiler_params=pltpu.CompilerParams(dimension_semantics=("arbitrary",),
                                             vmem_limit_bytes=VMEM_LIMIT_BYTES),
        name=f"layer{layer}",
    )(h, p["pre_mix_g"], p["w_in"], p["gmlp_ln_g"], p["gmlp_ln_b"], p["gmlp_ws"], p["gmlp_bs"],
      p["gmlp_out_g"], p["gla_gate_w"], p["gla_gate_b"], p["gla_out_g"], p["w_out"],
      p["post_mix_g"], p["pre_ff_g"], p["w_ff1"], p["w_ff2"], p["post_ff_g"])


def kernel(x, pre_mix_g, w_in, gmlp_ln_g, gmlp_ln_b, gmlp_ws, gmlp_bs, gmlp_out_g,
           gla_gate_w, gla_gate_b, gla_out_g, w_out, post_mix_g, pre_ff_g,
           w_ff1, w_ff2, post_ff_g):
    depth = w_in.shape[0]
    bsz, seq, _ = x.shape
    row = lambda g: g[:, None, :]
    pad = lambda w: jnp.pad(w, ((0, 0), (0, 0), (0, W_PAD)))
    p = {
        "pre_mix_g": row(pre_mix_g),
        "w_in": w_in.astype(BF16),
        "gmlp_ln_g": row(gmlp_ln_g),
        "gmlp_ln_b": row(gmlp_ln_b),
        "gmlp_ws": jnp.transpose(gmlp_ws, (0, 2, 1, 3)).reshape(depth, A_BLOCK, A_HEADS * A_BLOCK).astype(BF16),
        "gmlp_bs": jnp.repeat(jnp.transpose(gmlp_bs, (0, 2, 1)), A_HEAD_DIM, axis=2),
        "gmlp_out_g": row(gmlp_out_g),
        "gla_gate_w": gla_gate_w.astype(BF16),
        "gla_gate_b": row(gla_gate_b),
        "gla_out_g": row(gla_out_g),
        "w_out": pad(w_out.astype(BF16)),
        "post_mix_g": row(post_mix_g),
        "pre_ff_g": row(pre_ff_g),
        "w_ff1": pad(w_ff1.astype(BF16)),
        "w_ff2": pad(w_ff2.astype(BF16)),
        "post_ff_g": row(post_ff_g),
    }
    h = x.reshape(bsz * seq, D_MODEL)
    for layer in range(depth):
        h = _layer_call(layer, h, p, seq // TILE)
    return h.reshape(bsz, seq, D_MODEL)
```

```python
import functools

import jax
import jax.numpy as jnp
from jax import lax
from jax.experimental import pallas as pl
from jax.experimental.pallas import tpu as pltpu

D_MODEL = 1024
CHUNK = 64
D_A = 512
A_HEADS = 4
A_BLOCK = 128
A_HEAD_DIM = 128
B_HEADS = 4
B_DK = 256
B_DV = 512
B_HEAD_K = 64
B_HEAD_V = 128
B_GATE_RANK = 16
B_GATE_TAU = 16.0
D_FF = 4096
D_IN = 2 * D_A + 2 * B_DK + 2 * B_DV + B_GATE_RANK
EPS = 1e-6

C_UV = 0
C_QK = 2 * D_A
C_VG = C_QK + 2 * B_DK
C_LR = C_VG + 2 * B_DV

TILE = 512
FF_CHUNK = 512
N_FF_CHUNKS = D_FF // FF_CHUNK
W_PAD = 16
PREP_ROWS = 512
VMEM_LIMIT_BYTES = 56 * 1024 * 1024

F32 = jnp.float32
BF16 = jnp.bfloat16


def _rms(x, g):
    ms = jnp.mean(x * x, axis=-1, keepdims=True)
    return x * lax.rsqrt(ms + EPS) * g


def _gelu(x):
    return jax.nn.gelu(x, approximate=True)


def _dot(a, b):
    return jnp.dot(a, b, preferred_element_type=F32)


def _dot_nt(a, b):
    return lax.dot_general(a, b, (((1,), (1,)), ((), ())), preferred_element_type=F32)


def _split_bf16(x):
    hi = x.astype(BF16)
    lo = (x - hi.astype(F32)).astype(BF16)
    return hi, lo


def _block_diag_mask(rows, cols, row_blk, col_blk):
    r = lax.broadcasted_iota(jnp.int32, (rows, cols), 0) // row_blk
    c = lax.broadcasted_iota(jnp.int32, (rows, cols), 1) // col_blk
    return r == c


def _zeros_bf():
    return jnp.zeros((), BF16)


def _layer_kernel(tiles_per_seq,
                  h_ref, pre_g_ref, w_in_ref, ln_g_ref, ln_b_ref, ws_ref, bs_ref,
                  a_out_g_ref, gate_w_ref, gate_b_ref, b_out_g_ref, w_out_ref,
                  post_g_ref, ff_pre_g_ref, w1_ref, w2_ref, ff_post_g_ref,
                  o_ref, state_ref, h1_ref, y2_ref):
    step = pl.program_id(0)
    slot = step % 2

    @pl.when(step == 0)
    def _():
        h1_ref[1] = jnp.zeros(h1_ref.shape[1:], F32)
        y2_ref[1] = jnp.zeros(y2_ref.shape[1:], BF16)
        state_ref[...] = jnp.zeros_like(state_ref)

    y2 = y2_ref[1 - slot]
    ff = {"hidden": {}, "acc": None}

    def ffn_w1(j):
        f = jnp.dot(y2, w1_ref[:, j * FF_CHUNK:(j + 1) * FF_CHUNK], preferred_element_type=F32)
        f = jnp.maximum(f, 0.0)
        ff["hidden"][j] = (f * f).astype(BF16)

    def ffn_w2(i):
        f = jnp.concatenate([ff["hidden"].pop(2 * i), ff["hidden"].pop(2 * i + 1)], axis=1)
        part = jnp.dot(f, w2_ref[2 * i * FF_CHUNK:(2 * i + 2) * FF_CHUNK, :D_MODEL],
                       preferred_element_type=F32)
        ff["acc"] = part if ff["acc"] is None else ff["acc"] + part

    ffn_queue = []
    for i in range(N_FF_CHUNKS // 2):
        ffn_queue += [(ffn_w1, 2 * i), (ffn_w1, 2 * i + 1)]
        if i > 0:
            ffn_queue.insert(len(ffn_queue) - 1, (ffn_w2, i - 1))
    ffn_queue.append((ffn_w2, N_FF_CHUNKS // 2 - 1))

    def ffn_advance(n=1):
        for _ in range(min(n, len(ffn_queue))):
            fn, arg = ffn_queue.pop(0)
            fn(arg)

    ffn_advance()
    x = h_ref[...]
    y = _rms(x, pre_g_ref[...]).astype(BF16)
    z_lr = jnp.dot(y, w_in_ref[:, C_LR:D_IN], preferred_element_type=F32)
    z_qk = jnp.dot(y, w_in_ref[:, C_QK:C_VG], preferred_element_type=F32)
    gate = _dot(z_lr.astype(BF16), gate_w_ref[...]) + gate_b_ref[...]
    ffn_advance()
    log_a = jax.nn.log_sigmoid(gate) * (1.0 / B_GATE_TAU)
    n_chunks = TILE // CHUNK
    chunk_rows = [slice(c * CHUNK, (c + 1) * CHUNK) for c in range(n_chunks)]
    tri_bf = (lax.broadcasted_iota(jnp.int32, (CHUNK, CHUNK), 0)
              >= lax.broadcasted_iota(jnp.int32, (CHUNK, CHUNK), 1)).astype(BF16)
    la_hi, la_lo = _split_bf16(jnp.concatenate([log_a[r] for r in chunk_rows], axis=1))
    cum_all = _dot(tri_bf, la_hi) + _dot(tri_bf, la_lo)
    z_vg = jnp.dot(y, w_in_ref[:, C_VG:C_LR], preferred_element_type=F32)
    ffn_advance(2)

    causal = (lax.broadcasted_iota(jnp.int32, (CHUNK, B_HEADS * CHUNK), 0)
              >= lax.broadcasted_iota(jnp.int32, (CHUNK, B_HEADS * CHUNK), 1) % CHUNK)
    k_mask = _block_diag_mask(B_HEADS * CHUNK, B_DK, CHUNK, B_HEAD_K)
    pair_mask_v = _block_diag_mask(2 * CHUNK, 2 * B_HEAD_V, CHUNK, B_HEAD_V)
    qe, ke_t, kd_t, vv, dec_col = [], [], [], [], []
    for c, r in enumerate(chunk_rows):
        cum = cum_all[:, c * B_DK:(c + 1) * B_DK]
        last = cum[CHUNK - 1:CHUNK]
        qe.append((z_qk[r, :B_DK] * (B_HEAD_K ** -0.5) * jnp.exp(cum)).astype(BF16))
        ke = z_qk[r, B_DK:] * jnp.exp(-cum)
        ke_t.append(jnp.concatenate([ke] * B_HEADS, axis=0).T)
        vv.append(z_vg[r, :B_DV].astype(BF16))
        dec_col.append(jnp.broadcast_to(jnp.exp(last), (B_HEAD_V, B_DK)).T)
        kd_t.append((ke_t[c][:, :CHUNK] * dec_col[c][:, :CHUNK]).astype(BF16))
    upd = [jnp.concatenate(
        [_dot(kd_t[c][hd * B_HEAD_K:(hd + 1) * B_HEAD_K], vv[c][:, hd * B_HEAD_V:(hd + 1) * B_HEAD_V])
         for hd in range(B_HEADS)], axis=0) for c in range(n_chunks)]
    scores = []
    for c in range(n_chunks):
        ke_bd_t = jnp.where(k_mask, ke_t[c], 0.0).astype(BF16)
        scores.append(jnp.where(causal, _dot(qe[c], ke_bd_t), 0.0).astype(BF16))
    z_uv = jnp.dot(y, w_in_ref[:, C_UV:C_QK], preferred_element_type=F32)
    first_of_seq = (step % tiles_per_seq) == 0
    states = [jnp.where(first_of_seq, 0.0, state_ref[...])]
    for c in range(n_chunks):
        states.append(dec_col[c] * states[c] + upd[c])
    state_ref[...] = states[n_chunks]
    o_chunks = []
    for c in range(n_chunks):
        state_bf = states[c].astype(BF16)
        pair_out = []
        for p in range(B_HEADS // 2):
            jd = slice(p * 2 * CHUNK, (p + 1) * 2 * CHUNK)
            vp = vv[c][:, p * 2 * B_HEAD_V:(p + 1) * 2 * B_HEAD_V]
            v_bd = jnp.where(pair_mask_v, jnp.concatenate([vp, vp], axis=0), _zeros_bf())
            sp = state_bf[jd]
            s_bd = jnp.where(pair_mask_v, jnp.concatenate([sp, sp], axis=1), _zeros_bf())
            pair_out.append(_dot(jnp.concatenate([scores[c][:, jd], qe[c][:, jd]], axis=1),
                                 jnp.concatenate([v_bd, s_bd], axis=0)))
        o_chunks.append(jnp.concatenate(pair_out, axis=1))
    ffn_advance()
    o = jnp.concatenate(o_chunks, axis=0)
    heads = []
    for hd in range(B_HEADS):
        lanes = slice(hd * B_HEAD_V, (hd + 1) * B_HEAD_V)
        heads.append(_rms(o[:, lanes], b_out_g_ref[:, lanes]))
    zg = z_vg[:, B_DV:]
    b = (jnp.concatenate(heads, axis=1) * (zg * jax.nn.sigmoid(zg))).astype(BF16)

    u = _gelu(z_uv[:, :D_A])
    gv = _gelu(z_uv[:, D_A:])
    mu = jnp.mean(gv, axis=-1, keepdims=True)
    gc = gv - mu
    var = jnp.mean(gc * gc, axis=-1, keepdims=True)
    v = (gc * lax.rsqrt(var + EPS) * ln_g_ref[...] + ln_b_ref[...]).astype(BF16)
    ffn_advance(2)
    pos_i = lax.broadcasted_iota(jnp.int32, (A_BLOCK, A_HEADS * A_BLOCK), 0) // CHUNK
    pos_j = (lax.broadcasted_iota(jnp.int32, (A_BLOCK, A_HEADS * A_BLOCK), 1) % A_BLOCK) // CHUNK
    ws = jnp.where(pos_i >= pos_j, ws_ref[...], _zeros_bf())
    pair_mask = _block_diag_mask(2 * A_BLOCK, 2 * A_HEAD_DIM, A_BLOCK, A_HEAD_DIM)
    mixed_blocks = []
    for blk in range(TILE // A_BLOCK):
        vb = v[blk * A_BLOCK:(blk + 1) * A_BLOCK]
        pairs = []
        for p in range(A_HEADS // 2):
            vp = vb[:, p * 2 * A_HEAD_DIM:(p + 1) * 2 * A_HEAD_DIM]
            rhs = jnp.where(pair_mask, jnp.concatenate([vp, vp], axis=0), _zeros_bf())
            pairs.append(_dot(ws[:, p * 2 * A_BLOCK:(p + 1) * 2 * A_BLOCK], rhs))
        mixed_blocks.append(jnp.concatenate(pairs, axis=1) + bs_ref[...])
    ffn_advance()
    mixed = jnp.concatenate(mixed_blocks, axis=0)
    a = _rms(u * mixed, a_out_g_ref[...]).astype(BF16)
    ffn_advance(2)

    ab = jnp.concatenate([a, b], axis=1)
    half = TILE // 2
    for r in range(2):
        rows = slice(r * half, (r + 1) * half)
        m = jnp.dot(ab[rows], w_out_ref[:, :D_MODEL], preferred_element_type=F32)
        ffn_advance()
        h1 = x[rows] + _rms(m, post_g_ref[...])
        h1_ref[slot, rows] = h1
        y2_ref[slot, rows] = _rms(h1, ff_pre_g_ref[...]).astype(BF16)

    ffn_advance(len(ffn_queue))
    o_ref[...] = h1_ref[1 - slot] + _rms(ff["acc"], ff_post_g_ref[...])


def _cast_pad_kernel(w_ref, o_ref):
    n = w_ref.shape[-1]
    pad = o_ref.shape[-1] - n
    o_ref[:, :n] = w_ref[...].astype(BF16)
    if pad:
        o_ref[:, n:] = jnp.zeros((o_ref.shape[0], pad), BF16)


def _cast_pad_weight(w, pad):
    depth, k, n = w.shape
    return pl.pallas_call(
        _cast_pad_kernel,
        grid=(depth, k // PREP_ROWS),
        in_specs=[pl.BlockSpec((None, PREP_ROWS, n), lambda l, r: (l, r, 0))],
        out_specs=pl.BlockSpec((None, PREP_ROWS, n + pad), lambda l, r: (l, r, 0)),
        out_shape=jax.ShapeDtypeStruct((depth, k, n + pad), BF16),
        compiler_params=pltpu.CompilerParams(dimension_semantics=("arbitrary", "arbitrary")),
        name="cast_pad_weight",
    )(w)


def _layer_spec(layer, shape):
    zeros = (0,) * len(shape)
    return pl.BlockSpec((None,) + tuple(shape), lambda s: (layer,) + zeros,
                        pipeline_mode=pl.Buffered(1))


def _layer_call(layer, h, p, tiles_per_seq):
    n_tiles = h.shape[0] // TILE
    in_specs = [
        pl.BlockSpec((TILE, D_MODEL), lambda s: (jnp.minimum(s, n_tiles - 1), 0)),
        _layer_spec(layer, (1, D_MODEL)),
        _layer_spec(layer, (D_MODEL, D_IN)),
        _layer_spec(layer, (1, D_A)),
        _layer_spec(layer, (1, D_A)),
        _layer_spec(layer, (A_BLOCK, A_HEADS * A_BLOCK)),
        _layer_spec(layer, (A_BLOCK, D_A)),
        _layer_spec(layer, (1, D_A)),
        _layer_spec(layer, (B_GATE_RANK, B_DK)),
        _layer_spec(layer, (1, B_DK)),
        _layer_spec(layer, (1, B_DV)),
        _layer_spec(layer, (D_MODEL, D_MODEL + W_PAD)),
        _layer_spec(layer, (1, D_MODEL)),
        _layer_spec(layer, (1, D_MODEL)),
        _layer_spec(layer, (D_MODEL, D_FF + W_PAD)),
        _layer_spec(layer, (D_FF, D_MODEL + W_PAD)),
        _layer_spec(layer, (1, D_MODEL)),
    ]
    return pl.pallas_call(
        functools.partial(_layer_kernel, tiles_per_seq),
        grid=(n_tiles + 1,),
        in_specs=in_specs,
        out_specs=pl.BlockSpec((TILE, D_MODEL), lambda s: (jnp.maximum(s - 1, 0), 0)),
        out_shape=jax.ShapeDtypeStruct(h.shape, h.dtype),
        scratch_shapes=[pltpu.VMEM((B_DK, B_HEAD_V), F32),
                        pltpu.VMEM((2, TILE, D_MODEL), F32),
                        pltpu.VMEM((2, TILE, D_MODEL), BF16)],
        compiler_params=pltpu.CompilerParams(dimension_semantics=("arbitrary",),
                                             vmem_limit_bytes=VMEM_LIMIT_BYTES),
        name=f"layer{layer}",
    )(h, p["pre_mix_g"], p["w_in"], p["gmlp_ln_g"], p["gmlp_ln_b"], p["gmlp_ws"], p["gmlp_bs"],
      p["gmlp_out_g"], p["gla_gate_w"], p["gla_gate_b"], p["gla_out_g"], p["w_out"],
      p["post_mix_g"], p["pre_ff_g"], p["w_ff1"], p["w_ff2"], p["post_ff_g"])


def kernel(x, pre_mix_g, w_in, gmlp_ln_g, gmlp_ln_b, gmlp_ws, gmlp_bs, gmlp_out_g,
           gla_gate_w, gla_gate_b, gla_out_g, w_out, post_mix_g, pre_ff_g,
           w_ff1, w_ff2, post_ff_g):
    depth = w_in.shape[0]
    bsz, seq, _ = x.shape
    row = lambda g: g[:, None, :]
    p = {
        "pre_mix_g": row(pre_mix_g),
        "w_in": _cast_pad_weight(w_in, 0),
        "gmlp_ln_g": row(gmlp_ln_g),
        "gmlp_ln_b": row(gmlp_ln_b),
        "gmlp_ws": jnp.transpose(gmlp_ws, (0, 2, 1, 3)).reshape(depth, A_BLOCK, A_HEADS * A_BLOCK).astype(BF16),
        "gmlp_bs": jnp.repeat(jnp.transpose(gmlp_bs, (0, 2, 1)), A_HEAD_DIM, axis=2),
        "gmlp_out_g": row(gmlp_out_g),
        "gla_gate_w": gla_gate_w.astype(BF16),
        "gla_gate_b": row(gla_gate_b),
        "gla_out_g": row(gla_out_g),
        "w_out": _cast_pad_weight(w_out, W_PAD),
        "post_mix_g": row(post_mix_g),
        "pre_ff_g": row(pre_ff_g),
        "w_ff1": _cast_pad_weight(w_ff1, W_PAD),
        "w_ff2": _cast_pad_weight(w_ff2, W_PAD),
        "post_ff_g": row(post_ff_g),
    }
    h = x.reshape(bsz * seq, D_MODEL)
    for layer in range(depth):
        h = _layer_call(layer, h, p, seq // TILE)
    return h.reshape(bsz, seq, D_MODEL)
```

```python
import functools

import jax
import jax.numpy as jnp
from jax import lax
from jax.experimental import pallas as pl
from jax.experimental.pallas import tpu as pltpu

D_MODEL = 1024
CHUNK = 64
D_A = 512
A_HEADS = 4
A_BLOCK = 128
A_HEAD_DIM = 128
B_HEADS = 4
B_DK = 256
B_DV = 512
B_HEAD_K = 64
B_HEAD_V = 128
B_GATE_RANK = 16
B_GATE_TAU = 16.0
D_FF = 4096
D_IN = 2 * D_A + 2 * B_DK + 2 * B_DV + B_GATE_RANK
EPS = 1e-6

C_UV = 0
C_QK = 2 * D_A
C_VG = C_QK + 2 * B_DK
C_LR = C_VG + 2 * B_DV

TILE = 512
FF_CHUNK = 512
N_FF_CHUNKS = D_FF // FF_CHUNK
W_PAD = 16
PREP_ROWS = 512
VMEM_LIMIT_BYTES = 56 * 1024 * 1024

F32 = jnp.float32
BF16 = jnp.bfloat16


def _rms(x, g):
    ms = jnp.mean(x * x, axis=-1, keepdims=True)
    return x * lax.rsqrt(ms + EPS) * g


def _gelu(x):
    return jax.nn.gelu(x, approximate=True)


def _dot(a, b):
    return jnp.dot(a, b, preferred_element_type=F32)


def _dot_nt(a, b):
    return lax.dot_general(a, b, (((1,), (1,)), ((), ())), preferred_element_type=F32)


def _split_bf16(x):
    hi = x.astype(BF16)
    lo = (x - hi.astype(F32)).astype(BF16)
    return hi, lo


def _block_diag_mask(rows, cols, row_blk, col_blk):
    r = lax.broadcasted_iota(jnp.int32, (rows, cols), 0) // row_blk
    c = lax.broadcasted_iota(jnp.int32, (rows, cols), 1) // col_blk
    return r == c


def _zeros_bf():
    return jnp.zeros((), BF16)


def _layer_kernel(tiles_per_seq,
                  h_ref, pre_g_ref, w_in_ref, ln_g_ref, ln_b_ref, ws_ref, bs_ref,
                  a_out_g_ref, gate_w_ref, gate_b_ref, b_out_g_ref, w_out_ref,
                  post_g_ref, ff_pre_g_ref, w1_ref, w2_ref, ff_post_g_ref,
                  o_ref, state_ref, h1_ref, y2_ref):
    step = pl.program_id(0)
    slot = step % 2

    @pl.when(step == 0)
    def _():
        h1_ref[1] = jnp.zeros(h1_ref.shape[1:], F32)
        y2_ref[1] = jnp.zeros(y2_ref.shape[1:], BF16)
        state_ref[...] = jnp.zeros_like(state_ref)

    y2 = y2_ref[1 - slot]
    ff = {"hidden": {}, "acc": None}

    def ffn_w1(j):
        f = jnp.dot(y2, w1_ref[:, j * FF_CHUNK:(j + 1) * FF_CHUNK], preferred_element_type=F32)
        f = jnp.maximum(f, 0.0)
        ff["hidden"][j] = (f * f).astype(BF16)

    def ffn_w2(i):
        f = jnp.concatenate([ff["hidden"].pop(2 * i), ff["hidden"].pop(2 * i + 1)], axis=1)
        part = jnp.dot(f, w2_ref[2 * i * FF_CHUNK:(2 * i + 2) * FF_CHUNK, :D_MODEL],
                       preferred_element_type=F32)
        ff["acc"] = part if ff["acc"] is None else ff["acc"] + part

    ffn_queue = []
    for i in range(N_FF_CHUNKS // 2):
        ffn_queue += [(ffn_w1, 2 * i), (ffn_w1, 2 * i + 1)]
        if i > 0:
            ffn_queue.insert(len(ffn_queue) - 1, (ffn_w2, i - 1))
    ffn_queue.append((ffn_w2, N_FF_CHUNKS // 2 - 1))

    def ffn_advance(n=1):
        for _ in range(min(n, len(ffn_queue))):
            fn, arg = ffn_queue.pop(0)
            fn(arg)

    ffn_advance()
    x = h_ref[...]
    y = _rms(x, pre_g_ref[...]).astype(BF16)
    z_lr = jnp.dot(y, w_in_ref[:, C_LR:D_IN], preferred_element_type=F32)
    z_qk = jnp.dot(y, w_in_ref[:, C_QK:C_VG], preferred_element_type=F32)
    gate = _dot(z_lr.astype(BF16), gate_w_ref[...]) + gate_b_ref[...]
    ffn_advance()
    log_a = jax.nn.log_sigmoid(gate) * (1.0 / B_GATE_TAU)
    n_chunks = TILE // CHUNK
    chunk_rows = [slice(c * CHUNK, (c + 1) * CHUNK) for c in range(n_chunks)]
    tri_bf = (lax.broadcasted_iota(jnp.int32, (CHUNK, CHUNK), 0)
              >= lax.broadcasted_iota(jnp.int32, (CHUNK, CHUNK), 1)).astype(BF16)
    la_hi, la_lo = _split_bf16(jnp.concatenate([log_a[r] for r in chunk_rows], axis=1))
    cum_all = _dot(tri_bf, la_hi) + _dot(tri_bf, la_lo)
    z_vg = jnp.dot(y, w_in_ref[:, C_VG:C_LR], preferred_element_type=F32)
    ffn_advance(2)

    causal = (lax.broadcasted_iota(jnp.int32, (CHUNK, B_HEADS * CHUNK), 0)
              >= lax.broadcasted_iota(jnp.int32, (CHUNK, B_HEADS * CHUNK), 1) % CHUNK)
    k_mask = _block_diag_mask(B_HEADS * CHUNK, B_DK, CHUNK, B_HEAD_K)
    pair_mask_v = _block_diag_mask(2 * CHUNK, 2 * B_HEAD_V, CHUNK, B_HEAD_V)
    qe, ke_t, kd_t, vv, dec_col = [], [], [], [], []
    for c, r in enumerate(chunk_rows):
        cum = cum_all[:, c * B_DK:(c + 1) * B_DK]
        last = cum[CHUNK - 1:CHUNK]
        qe.append((z_qk[r, :B_DK] * (B_HEAD_K ** -0.5) * jnp.exp(cum)).astype(BF16))
        ke = z_qk[r, B_DK:] * jnp.exp(-cum)
        ke_t.append(jnp.concatenate([ke] * B_HEADS, axis=0).T)
        vv.append(z_vg[r, :B_DV].astype(BF16))
        dec_col.append(jnp.broadcast_to(jnp.exp(last), (B_HEAD_V, B_DK)).T)
        kd_t.append((ke_t[c][:, :CHUNK] * dec_col[c][:, :CHUNK]).astype(BF16))
    upd = [jnp.concatenate(
        [_dot(kd_t[c][hd * B_HEAD_K:(hd + 1) * B_HEAD_K], vv[c][:, hd * B_HEAD_V:(hd + 1) * B_HEAD_V])
         for hd in range(B_HEADS)], axis=0) for c in range(n_chunks)]
    scores = []
    for c in range(n_chunks):
        ke_bd_t = jnp.where(k_mask, ke_t[c], 0.0).astype(BF16)
        scores.append(jnp.where(causal, _dot(qe[c], ke_bd_t), 0.0).astype(BF16))
    z_uv = jnp.dot(y, w_in_ref[:, C_UV:C_QK], preferred_element_type=F32)
    first_of_seq = (step % tiles_per_seq) == 0
    states = [jnp.where(first_of_seq, 0.0, state_ref[...])]
    for c in range(n_chunks):
        states.append(dec_col[c] * states[c] + upd[c])
    state_ref[...] = states[n_chunks]
    o_chunks = []
    for c in range(n_chunks):
        state_bf = states[c].astype(BF16)
        pair_out = []
        for p in range(B_HEADS // 2):
            jd = slice(p * 2 * CHUNK, (p + 1) * 2 * CHUNK)
            vp = vv[c][:, p * 2 * B_HEAD_V:(p + 1) * 2 * B_HEAD_V]
            v_bd = jnp.where(pair_mask_v, jnp.concatenate([vp, vp], axis=0), _zeros_bf())
            sp = state_bf[jd]
            s_bd = jnp.where(pair_mask_v, jnp.concatenate([sp, sp], axis=1), _zeros_bf())
            pair_out.append(_dot(jnp.concatenate([scores[c][:, jd], qe[c][:, jd]], axis=1),
                                 jnp.concatenate([v_bd, s_bd], axis=0)))
        o_chunks.append(jnp.concatenate(pair_out, axis=1))
    ffn_advance()
    o = jnp.concatenate(o_chunks, axis=0)
    heads = []
    for hd in range(B_HEADS):
        lanes = slice(hd * B_HEAD_V, (hd + 1) * B_HEAD_V)
        heads.append(_rms(o[:, lanes], b_out_g_ref[:, lanes]))
    zg = z_vg[:, B_DV:]
    b = (jnp.concatenate(heads, axis=1) * (zg * jax.nn.sigmoid(zg))).astype(BF16)

    u = _gelu(z_uv[:, :D_A])
    gv = _gelu(z_uv[:, D_A:])
    mu = jnp.mean(gv, axis=-1, keepdims=True)
    gc = gv - mu
    var = jnp.mean(gc * gc, axis=-1, keepdims=True)
    v = (gc * lax.rsqrt(var + EPS) * ln_g_ref[...] + ln_b_ref[...]).astype(BF16)
    ffn_advance(2)
    pos_i = lax.broadcasted_iota(jnp.int32, (A_BLOCK, A_HEADS * A_BLOCK), 0) // CHUNK
    pos_j = (lax.broadcasted_iota(jnp.int32, (A_BLOCK, A_HEADS * A_BLOCK), 1) % A_BLOCK) // CHUNK
    ws = jnp.where(pos_i >= pos_j, ws_ref[...], _zeros_bf())
    pair_mask = _block_diag_mask(2 * A_BLOCK, 2 * A_HEAD_DIM, A_BLOCK, A_HEAD_DIM)
    mixed_blocks = []
    for blk in range(TILE // A_BLOCK):
        vb = v[blk * A_BLOCK:(blk + 1) * A_BLOCK]
        pairs = []
        for p in range(A_HEADS // 2):
            vp = vb[:, p * 2 * A_HEAD_DIM:(p + 1) * 2 * A_HEAD_DIM]
            rhs = jnp.where(pair_mask, jnp.concatenate([vp, vp], axis=0), _zeros_bf())
            pairs.append(_dot(ws[:, p * 2 * A_BLOCK:(p + 1) * 2 * A_BLOCK], rhs))
        mixed_blocks.append(jnp.concatenate(pairs, axis=1) + bs_ref[...])
    ffn_advance()
    mixed = jnp.concatenate(mixed_blocks, axis=0)
    a = _rms(u * mixed, a_out_g_ref[...]).astype(BF16)
    ffn_advance(2)

    ab = jnp.concatenate([a, b], axis=1)
    half = TILE // 2
    for r in range(2):
        rows = slice(r * half, (r + 1) * half)
        m = jnp.dot(ab[rows], w_out_ref[:, :D_MODEL], preferred_element_type=F32)
        ffn_advance()
        h1 = x[rows] + _rms(m, post_g_ref[...])
        h1_ref[slot, rows] = h1
        y2_ref[slot, rows] = _rms(h1, ff_pre_g_ref[...]).astype(BF16)

    ffn_advance(len(ffn_queue))
    o_ref[...] = h1_ref[1 - slot] + _rms(ff["acc"], ff_post_g_ref[...])


def _cast_pad_kernel(w_ref, o_ref):
    n = w_ref.shape[-1]
    pad = o_ref.shape[-1] - n
    o_ref[:, :n] = w_ref[...].astype(BF16)
    if pad:
        o_ref[:, n:] = jnp.zeros((o_ref.shape[0], pad), BF16)


def _cast_pad_weight(w, pad):
    depth, k, n = w.shape
    return pl.pallas_call(
        _cast_pad_kernel,
        grid=(depth, k // PREP_ROWS),
        in_specs=[pl.BlockSpec((None, PREP_ROWS, n), lambda l, r: (l, r, 0))],
        out_specs=pl.BlockSpec((None, PREP_ROWS, n + pad), lambda l, r: (l, r, 0)),
        out_shape=jax.ShapeDtypeStruct((depth, k, n + pad), BF16),
        compiler_params=pltpu.CompilerParams(dimension_semantics=("arbitrary", "arbitrary")),
        name="cast_pad_weight",
    )(w)


def _transpose_cast_kernel(wt_ref, o_ref):
    o_ref[...] = wt_ref[...].T.astype(BF16)


def _transpose_cast_weight(wt):
    depth, n, k = wt.shape
    return pl.pallas_call(
        _transpose_cast_kernel,
        grid=(depth, pl.cdiv(n, PREP_ROWS)),
        in_specs=[pl.BlockSpec((None, PREP_ROWS, k), lambda l, c: (l, c, 0))],
        out_specs=pl.BlockSpec((None, k, PREP_ROWS), lambda l, c: (l, 0, c)),
        out_shape=jax.ShapeDtypeStruct((depth, k, n), BF16),
        compiler_params=pltpu.CompilerParams(dimension_semantics=("arbitrary", "arbitrary")),
        name="transpose_cast_weight",
    )(wt)


def _layer_spec(layer, shape):
    zeros = (0,) * len(shape)
    return pl.BlockSpec((None,) + tuple(shape), lambda s: (layer,) + zeros,
                        pipeline_mode=pl.Buffered(1))


def _layer_call(layer, h, p, tiles_per_seq):
    n_tiles = h.shape[0] // TILE
    in_specs = [
        pl.BlockSpec((TILE, D_MODEL), lambda s: (jnp.minimum(s, n_tiles - 1), 0)),
        _layer_spec(layer, (1, D_MODEL)),
        _layer_spec(layer, (D_MODEL, D_IN)),
        _layer_spec(layer, (1, D_A)),
        _layer_spec(layer, (1, D_A)),
        _layer_spec(layer, (A_BLOCK, A_HEADS * A_BLOCK)),
        _layer_spec(layer, (A_BLOCK, D_A)),
        _layer_spec(layer, (1, D_A)),
        _layer_spec(layer, (B_GATE_RANK, B_DK)),
        _layer_spec(layer, (1, B_DK)),
        _layer_spec(layer, (1, B_DV)),
        _layer_spec(layer, (D_MODEL, D_MODEL + W_PAD)),
        _layer_spec(layer, (1, D_MODEL)),
        _layer_spec(layer, (1, D_MODEL)),
        _layer_spec(layer, (D_MODEL, D_FF + W_PAD)),
        _layer_spec(layer, (D_FF, D_MODEL + W_PAD)),
        _layer_spec(layer, (1, D_MODEL)),
    ]
    return pl.pallas_call(
        functools.partial(_layer_kernel, tiles_per_seq),
        grid=(n_tiles + 1,),
        in_specs=in_specs,
        out_specs=pl.BlockSpec((TILE, D_MODEL), lambda s: (jnp.maximum(s - 1, 0), 0)),
        out_shape=jax.ShapeDtypeStruct(h.shape, h.dtype),
        scratch_shapes=[pltpu.VMEM((B_DK, B_HEAD_V), F32),
                        pltpu.VMEM((2, TILE, D_MODEL), F32),
                        pltpu.VMEM((2, TILE, D_MODEL), BF16)],
        compiler_params=pltpu.CompilerParams(dimension_semantics=("arbitrary",),
                                             vmem_limit_bytes=VMEM_LIMIT_BYTES),
        name=f"layer{layer}",
    )(h, p["pre_mix_g"], p["w_in"], p["gmlp_ln_g"], p["gmlp_ln_b"], p["gmlp_ws"], p["gmlp_bs"],
      p["gmlp_out_g"], p["gla_gate_w"], p["gla_gate_b"], p["gla_out_g"], p["w_out"],
      p["post_mix_g"], p["pre_ff_g"], p["w_ff1"], p["w_ff2"], p["post_ff_g"])


def kernel(x, pre_mix_g, w_in, gmlp_ln_g, gmlp_ln_b, gmlp_ws, gmlp_bs, gmlp_out_g,
           gla_gate_w, gla_gate_b, gla_out_g, w_out, post_mix_g, pre_ff_g,
           w_ff1, w_ff2, post_ff_g):
    depth = w_in.shape[0]
    bsz, seq, _ = x.shape
    row = lambda g: g[:, None, :]
    p = {
        "pre_mix_g": row(pre_mix_g),
        "w_in": _transpose_cast_weight(jnp.swapaxes(w_in, 1, 2)),
        "gmlp_ln_g": row(gmlp_ln_g),
        "gmlp_ln_b": row(gmlp_ln_b),
        "gmlp_ws": jnp.transpose(gmlp_ws, (0, 2, 1, 3)).reshape(depth, A_BLOCK, A_HEADS * A_BLOCK).astype(BF16),
        "gmlp_bs": jnp.repeat(jnp.transpose(gmlp_bs, (0, 2, 1)), A_HEAD_DIM, axis=2),
        "gmlp_out_g": row(gmlp_out_g),
        "gla_gate_w": gla_gate_w.astype(BF16),
        "gla_gate_b": row(gla_gate_b),
        "gla_out_g": row(gla_out_g),
        "w_out": _cast_pad_weight(w_out, W_PAD),
        "post_mix_g": row(post_mix_g),
        "pre_ff_g": row(pre_ff_g),
        "w_ff1": _cast_pad_weight(w_ff1, W_PAD),
        "w_ff2": _cast_pad_weight(w_ff2, W_PAD),
        "post_ff_g": row(post_ff_g),
    }
    h = x.reshape(bsz * seq, D_MODEL)
    for layer in range(depth):
        h = _layer_call(layer, h, p, seq // TILE)
    return h.reshape(bsz, seq, D_MODEL)
```
